```python
import jax, jax.numpy as jnp
from jax import lax
import numpy as np

D_MODEL = 2048
BATCH = 4
SEQ = 2048
DEPTH = 1

GRID_W = 64
CTX_LEN = 256
HEAD_DIM = 128
N_Q_HEADS = 12
N_KV_HEADS = 4
Q_PER_KV = N_Q_HEADS // N_KV_HEADS
N_FOURIER_GROUPS = 4
FOURIER_GROUP_DIM = 128
ATTN_WIDTH = N_Q_HEADS * HEAD_DIM
KV_WIDTH = N_KV_HEADS * HEAD_DIM
FOURIER_WIDTH = N_FOURIER_GROUPS * FOURIER_GROUP_DIM
IN_WIDTH = ATTN_WIDTH + 2 * KV_WIDTH + FOURIER_WIDTH
MIX_WIDTH = ATTN_WIDTH + FOURIER_WIDTH
D_FF = 4 * D_MODEL
Q_BLOCK = 128
ROPE_THETA = 10000.0
ROPE_AXIS_DIM = HEAD_DIM // 2
N_MOD = 6
EPS = 1e-6

kernel_name = "hybrid_gqa_fourier_dit_prefix_layer"


def rmsnorm(x, g):
    xf = x.astype(jnp.float32)
    y = xf * lax.rsqrt(jnp.mean(xf * xf, axis=-1, keepdims=True) + EPS)
    return (y * g.astype(jnp.float32)).astype(x.dtype)


def modulate(h, shift, scale):
    return h * (1.0 + scale) + shift


def axial_rope_tables(rows):
    t = jnp.arange(rows * GRID_W)
    row = (t // GRID_W).astype(jnp.float32)
    col = (t % GRID_W).astype(jnp.float32)
    inv = ROPE_THETA ** (-jnp.arange(0, ROPE_AXIS_DIM, 2, dtype=jnp.float32) / ROPE_AXIS_DIM)
    ang_r = row[:, None, None] * inv
    ang_c = col[:, None, None] * inv
    return (jnp.cos(ang_r), jnp.sin(ang_r), jnp.cos(ang_c), jnp.sin(ang_c))


def rotate(x, cos, sin):
    half = x.shape[-1] // 2
    x1, x2 = x[..., :half], x[..., half:]
    return jnp.concatenate([x1 * cos - x2 * sin, x2 * cos + x1 * sin], axis=-1)


def apply_axial_rope(x, tabs):
    cr, sr, cc, sc = tabs
    xf = x.astype(jnp.float32)
    y = jnp.concatenate([rotate(xf[..., :ROPE_AXIS_DIM], cr, sr),
                         rotate(xf[..., ROPE_AXIS_DIM:], cc, sc)], axis=-1)
    return y.astype(x.dtype)


def heads(t, n):
    return t.reshape(t.shape[:-1] + (n, HEAD_DIM))


def gqa_scores_out(qblk, k_f, v_f):
    s = jnp.einsum('bqkgd,bnkd->bkgqn', qblk.astype(jnp.float32), k_f) * (HEAD_DIM ** -0.5)
    p = jax.nn.softmax(s, axis=-1)
    return jnp.einsum('bkgqn,bnkd->bqkgd', p, v_f)


def latent_attention(q, k, v, kc, vc):
    B, S = q.shape[0], q.shape[1]
    nblk = S // Q_BLOCK
    k_f = jnp.concatenate([kc, k], axis=1).astype(jnp.float32)
    v_f = jnp.concatenate([vc, v], axis=1).astype(jnp.float32)
    qb = q.reshape(B, nblk, Q_BLOCK, N_KV_HEADS, Q_PER_KV, HEAD_DIM)
    qb = jnp.moveaxis(qb, 1, 0)
    out = lax.map(lambda qblk: gqa_scores_out(qblk, k_f, v_f), qb)
    out = jnp.moveaxis(out, 0, 1).reshape(B, S, ATTN_WIDTH)
    return out.astype(q.dtype)


def context_attention(qc, kc, vc):
    B, L = qc.shape[0], qc.shape[1]
    qb = qc.reshape(B, L, N_KV_HEADS, Q_PER_KV, HEAD_DIM)
    o = gqa_scores_out(qb, kc.astype(jnp.float32), vc.astype(jnp.float32))
    return o.reshape(B, L, ATTN_WIDTH).astype(qc.dtype)


def fourier_mix(u, w_f):
    B, N = u.shape[0], u.shape[1]
    ug = u.reshape(B, N, N_FOURIER_GROUPS, FOURIER_GROUP_DIM).astype(jnp.float32)
    f = jnp.fft.fft2(ug, axes=(1, 3), norm='ortho').real
    y = jnp.einsum('bngc,gcd->bngd', f, w_f.astype(jnp.float32))
    return y.reshape(B, N, FOURIER_WIDTH).astype(u.dtype)


def split_proj(p):
    q = p[..., :ATTN_WIDTH]
    k = p[..., ATTN_WIDTH:ATTN_WIDTH + KV_WIDTH]
    v = p[..., ATTN_WIDTH + KV_WIDTH:ATTN_WIDTH + 2 * KV_WIDTH]
    u = p[..., ATTN_WIDTH + 2 * KV_WIDTH:]
    return q, k, v, u


def sq_relu_mlp(h, w1, w2):
    a = jax.nn.relu(h @ w1)
    return (a * a) @ w2


def hybrid_layer(x, ctx, mod_lat, mod_ctx, g1, w_in, q_g, k_g, w_f, w_out, g2, w1, w2, tabs, update_ctx):
    sh1, sc1, gt1, sh2, sc2, gt2 = jnp.split(mod_lat, N_MOD, axis=-1)
    csh1, csc1, cgt1, csh2, csc2, cgt2 = jnp.split(mod_ctx, N_MOD, axis=-1)

    hc = modulate(rmsnorm(ctx, g1), csh1, csc1)
    if update_ctx:
        qc, kc, vc, uc = split_proj(hc @ w_in)
    else:
        kv = hc @ w_in[:, ATTN_WIDTH:ATTN_WIDTH + 2 * KV_WIDTH]
        kc, vc = kv[..., :KV_WIDTH], kv[..., KV_WIDTH:]
    kc = rmsnorm(heads(kc, N_KV_HEADS), k_g)
    vc = heads(vc, N_KV_HEADS)

    h = modulate(rmsnorm(x, g1), sh1, sc1)
    q, k, v, u = split_proj(h @ w_in)
    q = apply_axial_rope(rmsnorm(heads(q, N_Q_HEADS), q_g), tabs)
    k = apply_axial_rope(rmsnorm(heads(k, N_KV_HEADS), k_g), tabs)
    v = heads(v, N_KV_HEADS)
    attn = latent_attention(q, k, v, kc, vc)
    four = fourier_mix(u, w_f)
    x = x + gt1 * (jnp.concatenate([attn, four], axis=-1) @ w_out)

    h2 = modulate(rmsnorm(x, g2), sh2, sc2)
    x = x + gt2 * sq_relu_mlp(h2, w1, w2)

    if update_ctx:
        qc = rmsnorm(heads(qc, N_Q_HEADS), q_g)
        attn_c = context_attention(qc, kc, vc)
        four_c = fourier_mix(uc, w_f)
        ctx = ctx + cgt1 * (jnp.concatenate([attn_c, four_c], axis=-1) @ w_out)
        hc2 = modulate(rmsnorm(ctx, g2), csh2, csc2)
        ctx = ctx + cgt2 * sq_relu_mlp(hc2, w1, w2)
    return x, ctx


def setup_inputs(seed: int = 0) -> dict:
    key = jax.random.key(seed)
    ks = jax.random.split(key, 16)
    f32 = jnp.float32
    n = lambda k, shape, s: jax.random.normal(k, shape, f32) * s
    return {
        "x": n(ks[0], (BATCH, SEQ, D_MODEL), 1.0),
        "c": n(ks[1], (BATCH, D_MODEL), 1.0),
        "ctx": n(ks[2], (BATCH, CTX_LEN, D_MODEL), 1.0),
        "c_ctx": n(ks[3], (D_MODEL,), 1.0),
        "w_ada": n(ks[4], (DEPTH, D_MODEL, N_MOD * D_MODEL), D_MODEL ** -0.5),
        "b_ada": n(ks[5], (DEPTH, N_MOD * D_MODEL), 0.02),
        "norm1_g": 1.0 + n(ks[6], (DEPTH, D_MODEL), 0.02),
        "w_in": n(ks[7], (DEPTH, D_MODEL, IN_WIDTH), D_MODEL ** -0.5),
        "q_norm_g": 1.0 + n(ks[8], (DEPTH, HEAD_DIM), 0.02),
        "k_norm_g": 1.0 + n(ks[9], (DEPTH, HEAD_DIM), 0.02),
        "w_fourier": n(ks[10], (DEPTH, N_FOURIER_GROUPS, FOURIER_GROUP_DIM, FOURIER_GROUP_DIM), FOURIER_GROUP_DIM ** -0.5),
        "w_out": n(ks[11], (DEPTH, MIX_WIDTH, D_MODEL), MIX_WIDTH ** -0.5),
        "norm2_g": 1.0 + n(ks[12], (DEPTH, D_MODEL), 0.02),
        "w_mlp1": n(ks[13], (DEPTH, D_MODEL, D_FF), D_MODEL ** -0.5),
        "w_mlp2": n(ks[14], (DEPTH, D_FF, D_MODEL), D_FF ** -0.5),
        "final_norm_g": 1.0 + n(ks[15], (D_MODEL,), 0.02),
    }


def reference(x, c, ctx, c_ctx, w_ada, b_ada, norm1_g, w_in, q_norm_g, k_norm_g, w_fourier,
              w_out, norm2_g, w_mlp1, w_mlp2, final_norm_g):
    ROWS = x.shape[1] // GRID_W
    tabs = axial_rope_tables(ROWS)
    silu_c = jax.nn.silu(c)
    silu_cc = jax.nn.silu(c_ctx)
    for layer in range(DEPTH):
        mod_lat = (silu_c @ w_ada[layer] + b_ada[layer])[:, None, :]
        mod_ctx = (silu_cc @ w_ada[layer] + b_ada[layer])[None, None, :]
        x, ctx = hybrid_layer(x, ctx, mod_lat, mod_ctx, norm1_g[layer], w_in[layer],
                              q_norm_g[layer], k_norm_g[layer], w_fourier[layer], w_out[layer],
                              norm2_g[layer], w_mlp1[layer], w_mlp2[layer], tabs,
                              update_ctx=(layer < DEPTH - 1))
    return rmsnorm(x, final_norm_g)
```

```python
import functools

import numpy as np
import jax
import jax.numpy as jnp
from jax import lax
from jax.experimental import pallas as pl
from jax.experimental.pallas import tpu as pltpu

F32 = jnp.float32
BF16 = jnp.bfloat16

D_MODEL = 2048
GRID_W = 64
HEAD_DIM = 128
N_Q_HEADS = 12
N_KV_HEADS = 4
Q_PER_KV = N_Q_HEADS // N_KV_HEADS
N_GROUPS = 4
GROUP_DIM = 128
ATTN_WIDTH = N_Q_HEADS * HEAD_DIM
KV_WIDTH = N_KV_HEADS * HEAD_DIM
FOURIER_WIDTH = N_GROUPS * GROUP_DIM
IN_WIDTH = ATTN_WIDTH + 2 * KV_WIDTH + FOURIER_WIDTH
D_FF = 4 * D_MODEL
ROPE_THETA = 10000.0
ROPE_AXIS_DIM = HEAD_DIM // 2
N_MOD = 6
EPS = 1e-6

V7X_VMEM_BYTES = 64 * 1024 * 1024
VMEM_LIMIT = V7X_VMEM_BYTES * 7 // 8
MOD_ROWS = 8
CTX_MOD_ROW = 4
DFT_LO = 32


def _params(*sem):
    return pltpu.CompilerParams(dimension_semantics=sem, vmem_limit_bytes=VMEM_LIMIT)


def _resident(shape):
    return pl.BlockSpec(shape, lambda *_: (0,) * len(shape), pipeline_mode=pl.Buffered(1))


def _dot(a, b):
    return jnp.dot(a, b, preferred_element_type=F32)


def _rms(x):
    return x * lax.rsqrt(jnp.mean(x * x, axis=-1, keepdims=True) + EPS)


def _ada_kernel(c_ref, w_ref, b_ref, o_ref):
    c = c_ref[...]
    a = c / (1.0 + jnp.exp(-c))
    o_ref[...] = _dot(a.astype(BF16), w_ref[...].astype(BF16)) + b_ref[...]


def _ada(c_rows, w_ada, b_ada):
    tn = 1024
    n = w_ada.shape[1]
    return pl.pallas_call(
        _ada_kernel,
        out_shape=jax.ShapeDtypeStruct((MOD_ROWS, n), F32),
        grid=(n // tn,),
        in_specs=[pl.BlockSpec((MOD_ROWS, D_MODEL), lambda j: (0, 0)),
                  pl.BlockSpec((D_MODEL, tn), lambda j: (0, j)),
                  pl.BlockSpec((1, tn), lambda j: (0, j))],
        out_specs=pl.BlockSpec((MOD_ROWS, tn), lambda j: (0, j)),
        compiler_params=_params("parallel"),
        name="ada",
    )(c_rows, w_ada, b_ada)


def _fw_kernel(cs_ref, wf_ref, o_ref, *, scale):
    r = _dot(cs_ref[...].astype(BF16), wf_ref[0].astype(BF16)) * scale
    o_ref[0, :, :GROUP_DIM] = r[:GROUP_DIM].astype(BF16)
    o_ref[0, :, GROUP_DIM:] = r[GROUP_DIM:].astype(BF16)


def _fourier_weights(cs, w_f, seq):
    scale = 1.0 / float(np.sqrt(seq * GROUP_DIM))
    return pl.pallas_call(
        functools.partial(_fw_kernel, scale=scale),
        out_shape=jax.ShapeDtypeStruct((N_GROUPS, GROUP_DIM, 2 * GROUP_DIM), BF16),
        grid=(N_GROUPS,),
        in_specs=[pl.BlockSpec((2 * GROUP_DIM, GROUP_DIM), lambda g: (0, 0)),
                  pl.BlockSpec((1, GROUP_DIM, GROUP_DIM), lambda g: (g, 0, 0))],
        out_specs=pl.BlockSpec((1, GROUP_DIM, 2 * GROUP_DIM), lambda g: (g, 0, 0)),
        compiler_params=_params("parallel"),
        name="fourier_weights",
    )(cs, w_f)


def _norm_modulate(x, g, shift, scale):
    return (_rms(x) * g) * (1.0 + scale) + shift


def _ctx_kv_kernel(x_ref, sh_ref, sc_ref, g1_ref, kg_ref, w_ref, k_ref, v_ref):
    h = _norm_modulate(x_ref[0], g1_ref[...], sh_ref[0], sc_ref[0]).astype(BF16)
    pk = _dot(h, w_ref[:, :KV_WIDTH])
    kg = kg_ref[...]
    for j in range(N_KV_HEADS):
        sl = slice(j * HEAD_DIM, (j + 1) * HEAD_DIM)
        k_ref[0, :, sl] = (_rms(pk[:, sl]) * kg).astype(BF16)
    v_ref[0] = _dot(h, w_ref[:, KV_WIDTH:]).astype(BF16)


def _ctx_kv(ctx, mod3, g1, kg, w_kv):
    b, l, d = ctx.shape
    return pl.pallas_call(
        _ctx_kv_kernel,
        out_shape=(jax.ShapeDtypeStruct((b, l, KV_WIDTH), BF16),) * 2,
        grid=(b,),
        in_specs=[pl.BlockSpec((1, l, d), lambda i: (i, 0, 0)),
                  pl.BlockSpec((1, 1, d), lambda i: (CTX_MOD_ROW, 0, 0)),
                  pl.BlockSpec((1, 1, d), lambda i: (CTX_MOD_ROW, 0, 1)),
                  _resident((1, d)),
                  _resident((1, HEAD_DIM)),
                  _resident((d, 2 * KV_WIDTH))],
        out_specs=(pl.BlockSpec((1, l, KV_WIDTH), lambda i: (i, 0, 0)),) * 2,
        compiler_params=_params("parallel"),
        name="ctx_kv",
    )(ctx, mod3, mod3, g1, kg, w_kv)


def _in_proj_kernel(x_ref, sh_ref, sc_ref, g1_ref, cos_ref, sin_ref, qg_ref, kg_ref, w_ref, ab_ref,
                    q_ref, k_ref, v_ref, t_ref):
    h = _norm_modulate(x_ref[0], g1_ref[...], sh_ref[0], sc_ref[0]).astype(BF16)
    cos = cos_ref[...]
    sin = sin_ref[...]
    lane = lax.broadcasted_iota(jnp.int32, cos.shape, 1)
    first_half = (lane % ROPE_AXIS_DIM) < (ROPE_AXIS_DIM // 2)

    def norm_rope(p, g):
        y = _rms(p) * g
        partner = jnp.where(first_half,
                            pltpu.roll(y, HEAD_DIM - ROPE_AXIS_DIM // 2, 1),
                            pltpu.roll(y, ROPE_AXIS_DIM // 2, 1))
        return y * cos + partner * sin

    heads_per_chunk = KV_WIDTH // HEAD_DIM
    qg = qg_ref[...] * (HEAD_DIM ** -0.5)
    for c in range(ATTN_WIDTH // KV_WIDTH):
        p = _dot(h, w_ref[:, c * KV_WIDTH:(c + 1) * KV_WIDTH])
        for j in range(heads_per_chunk):
            sl = slice(j * HEAD_DIM, (j + 1) * HEAD_DIM)
            osl = slice(c * KV_WIDTH + j * HEAD_DIM, c * KV_WIDTH + (j + 1) * HEAD_DIM)
            q_ref[0, :, osl] = norm_rope(p[:, sl], qg).astype(BF16)
    p = _dot(h, w_ref[:, ATTN_WIDTH:ATTN_WIDTH + KV_WIDTH])
    kg = kg_ref[...]
    for j in range(N_KV_HEADS):
        sl = slice(j * HEAD_DIM, (j + 1) * HEAD_DIM)
        k_ref[0, :, sl] = norm_rope(p[:, sl], kg).astype(BF16)
    v_ref[0] = _dot(h, w_ref[:, ATTN_WIDTH + KV_WIDTH:ATTN_WIDTH + 2 * KV_WIDTH]).astype(BF16)
    u = _dot(h, w_ref[:, ATTN_WIDTH + 2 * KV_WIDTH:]).astype(BF16)
    for g in range(N_GROUPS):
        sl = slice(g * GROUP_DIM, (g + 1) * GROUP_DIM)
        t = _dot(u[:, sl], ab_ref[g])
        t_ref[0, 0, :, sl] = t[:, :GROUP_DIM].astype(BF16)
        t_ref[0, 1, :, sl] = t[:, GROUP_DIM:].astype(BF16)


def _in_proj(x, mod3, g1, cos, sin, qg, kg, w_in, ab, tm):
    b, s, d = x.shape
    row = lambda i, t: (i, t, 0)
    return pl.pallas_call(
        _in_proj_kernel,
        out_shape=(jax.ShapeDtypeStruct((b, s, ATTN_WIDTH), BF16),
                   jax.ShapeDtypeStruct((b, s, KV_WIDTH), BF16),
                   jax.ShapeDtypeStruct((b, s, KV_WIDTH), BF16),
                   jax.ShapeDtypeStruct((b, 2, s, FOURIER_WIDTH), BF16)),
        grid=(b, s // tm),
        in_specs=[pl.BlockSpec((1, tm, d), row),
                  pl.BlockSpec((1, 1, d), lambda i, t: (i, 0, 0)),
                  pl.BlockSpec((1, 1, d), lambda i, t: (i, 0, 1)),
                  _resident((1, d)),
                  pl.BlockSpec((tm, HEAD_DIM), lambda i, t: (t, 0)),
                  pl.BlockSpec((tm, HEAD_DIM), lambda i, t: (t, 0)),
                  _resident((1, HEAD_DIM)),
                  _resident((1, HEAD_DIM)),
                  _resident((d, IN_WIDTH)),
                  _resident((N_GROUPS, GROUP_DIM, 2 * GROUP_DIM))],
        out_specs=(pl.BlockSpec((1, tm, ATTN_WIDTH), row),
                   pl.BlockSpec((1, tm, KV_WIDTH), row),
                   pl.BlockSpec((1, tm, KV_WIDTH), row),
                   pl.BlockSpec((1, 2, tm, FOURIER_WIDTH), lambda i, t: (i, 0, t, 0))),
        compiler_params=_params("parallel", "parallel"),
        name="in_proj",
    )(x, mod3, mod3, g1, cos, sin, qg, kg, w_in, ab)


def _attn_kernel(q_ref, kc_ref, k_ref, vc_ref, v_ref, o_ref):
    nt = (((1,), (1,)), ((), ()))
    kc = kc_ref[0]
    k = k_ref[0]
    vc = vc_ref[0]
    v = v_ref[0]
    for g in range(Q_PER_KV):
        sl = slice(g * HEAD_DIM, (g + 1) * HEAD_DIM)
        q = q_ref[0, :, sl]
        s_c = lax.dot_general(q, kc, nt, preferred_element_type=F32)
        s_l = lax.dot_general(q, k, nt, preferred_element_type=F32)
        m = jnp.maximum(jnp.max(s_c, axis=-1, keepdims=True), jnp.max(s_l, axis=-1, keepdims=True))
        p_c = jnp.exp(s_c - m)
        p_l = jnp.exp(s_l - m)
        denom = jnp.sum(p_c, axis=-1, keepdims=True) + jnp.sum(p_l, axis=-1, keepdims=True)
        o = _dot(p_c.astype(BF16), vc) + _dot(p_l.astype(BF16), v)
        o_ref[0, :, sl] = (o / denom).astype(BF16)


def _attention(q, kc, k, vc, v, tq):
    b, s, _ = q.shape
    l = kc.shape[1]
    gw = Q_PER_KV * HEAD_DIM
    kv = lambda n: pl.BlockSpec((1, n, HEAD_DIM), lambda i, h, t: (i, 0, h))
    return pl.pallas_call(
        _attn_kernel,
        out_shape=jax.ShapeDtypeStruct((b, s, ATTN_WIDTH), BF16),
        grid=(b, N_KV_HEADS, s // tq),
        in_specs=[pl.BlockSpec((1, tq, gw), lambda i, h, t: (i, t, h)),
                  kv(l), kv(s), kv(l), kv(s)],
        out_specs=pl.BlockSpec((1, tq, gw), lambda i, h, t: (i, t, h)),
        compiler_params=_params("parallel", "parallel", "parallel"),
        name="attn",
    )(q, kc, k, vc, v)


def _dft_kernel(c1_ref, s1_ref, c2_ref, s2_ref, t_ref, o_ref, tab_ref):
    seq = c2_ref.shape[1]
    c2 = c2_ref[...]
    s2 = s2_ref[...]
    for r in range(c1_ref.shape[0]):
        c1 = c1_ref[r:r + 1, :]
        s1 = s1_ref[r:r + 1, :]
        rows = slice(r * DFT_LO, (r + 1) * DFT_LO)
        tab_ref[rows, :seq] = (c1 * c2 - s1 * s2).astype(BF16)
        tab_ref[rows, seq:] = (-(s1 * c2 + c1 * s2)).astype(BF16)
    tab = tab_ref[...]
    for i in range(t_ref.shape[0]):
        o_ref[i] = _dot(tab, t_ref[i]).astype(BF16)


def _dft(c1, s1, c2, s2, t, tn):
    b, two_s, w = t.shape
    s = two_s // 2
    hi = tn // DFT_LO
    return pl.pallas_call(
        _dft_kernel,
        out_shape=jax.ShapeDtypeStruct((b, s, w), BF16),
        grid=(s // tn,),
        in_specs=[pl.BlockSpec((hi, s), lambda i: (i, 0)),
                  pl.BlockSpec((hi, s), lambda i: (i, 0)),
                  _resident((DFT_LO, s)),
                  _resident((DFT_LO, s)),
                  _resident((b, two_s, w))],
        out_specs=pl.BlockSpec((b, tn, w), lambda i: (0, i, 0)),
        scratch_shapes=[pltpu.VMEM((tn, two_s), BF16)],
        compiler_params=_params("parallel"),
        name="dft",
    )(c1, s1, c2, s2, t)


def _out_proj_kernel(a_ref, f_ref, x_ref, gt_ref, sh_ref, sc_ref, g2_ref, w_ref, x1_ref, h2_ref):
    y = _dot(a_ref[0], w_ref[:ATTN_WIDTH, :]) + _dot(f_ref[0], w_ref[ATTN_WIDTH:, :])
    x1 = x_ref[0] + gt_ref[0] * y
    x1_ref[0] = x1
    h2_ref[0] = _norm_modulate(x1, g2_ref[...], sh_ref[0], sc_ref[0]).astype(BF16)


def _out_proj(attn, four, x, mod3, g2, w_out, tm):
    b, s, d = x.shape
    row = lambda i, t: (i, t, 0)
    mod = lambda j: pl.BlockSpec((1, 1, d), lambda i, t: (i, 0, j))
    return pl.pallas_call(
        _out_proj_kernel,
        out_shape=(jax.ShapeDtypeStruct((b, s, d), F32),
                   jax.ShapeDtypeStruct((b, s, d), BF16)),
        grid=(b, s // tm),
        in_specs=[pl.BlockSpec((1, tm, ATTN_WIDTH), row),
                  pl.BlockSpec((1, tm, FOURIER_WIDTH), row),
                  pl.BlockSpec((1, tm, d), row),
                  mod(2), mod(3), mod(4),
                  _resident((1, d)),
                  _resident((ATTN_WIDTH + FOURIER_WIDTH, d))],
        out_specs=(pl.BlockSpec((1, tm, d), row),
                   pl.BlockSpec((1, tm, d), row)),
        compiler_params=_params("parallel", "parallel"),
        name="out_proj",
    )(attn, four, x, mod3, mod3, mod3, g2, w_out)


def _mlp_kernel(h_ref, x1_ref, gt_ref, gf_ref, w1_ref, w2_ref, o_ref, acc_ref):
    f = pl.program_id(2)

    @pl.when(f == 0)
    def _():
        acc_ref[...] = jnp.zeros_like(acc_ref)

    a = jnp.maximum(_dot(h_ref[0], w1_ref[...]), 0.0)
    acc_ref[...] += _dot((a * a).astype(BF16), w2_ref[...])

    @pl.when(f == pl.num_programs(2) - 1)
    def _():
        x2 = x1_ref[0] + gt_ref[0] * acc_ref[...]
        o_ref[0] = _rms(x2) * gf_ref[...]


def _mlp(h2, x1, mod3, gf, w1, w2, tm, tf):
    b, s, d = x1.shape
    row = lambda i, t, f: (i, t, 0)
    return pl.pallas_call(
        _mlp_kernel,
        out_shape=jax.ShapeDtypeStruct((b, s, d), F32),
        grid=(b, s // tm, D_FF // tf),
        in_specs=[pl.BlockSpec((1, tm, d), row),
                  pl.BlockSpec((1, tm, d), row),
                  pl.BlockSpec((1, 1, d), lambda i, t, f: (i, 0, 5)),
                  pl.BlockSpec((1, d), lambda i, t, f: (0, 0)),
                  pl.BlockSpec((d, tf), lambda i, t, f: (0, f)),
                  pl.BlockSpec((tf, d), lambda i, t, f: (f, 0))],
        out_specs=pl.BlockSpec((1, tm, d), row),
        scratch_shapes=[pltpu.VMEM((tm, d), F32)],
        compiler_params=_params("parallel", "parallel", "arbitrary"),
        name="mlp",
    )(h2, x1, mod3, gf, w1, w2)


def _rope_tables(seq):
    t = np.arange(seq)
    inv = ROPE_THETA ** (-np.arange(0, ROPE_AXIS_DIM, 2, dtype=np.float64) / ROPE_AXIS_DIM)
    ang_r = (t // GRID_W)[:, None] * inv
    ang_c = (t % GRID_W)[:, None] * inv
    cos = np.concatenate([np.cos(ang_r), np.cos(ang_r), np.cos(ang_c), np.cos(ang_c)], axis=-1)
    sin = np.concatenate([-np.sin(ang_r), np.sin(ang_r), -np.sin(ang_c), np.sin(ang_c)], axis=-1)
    return jnp.asarray(cos, F32), jnp.asarray(sin, F32)


def _dft_tables(seq):
    m = np.arange(seq)
    a1 = 2.0 * np.pi * ((DFT_LO * np.arange(seq // DFT_LO)[:, None] * m) % seq) / seq
    a2 = 2.0 * np.pi * ((np.arange(DFT_LO)[:, None] * m) % seq) / seq
    c = np.arange(GROUP_DIM)
    ac = 2.0 * np.pi * ((c[:, None] * c) % GROUP_DIM) / GROUP_DIM
    cs = np.concatenate([np.cos(ac), np.sin(ac)], axis=0)
    f = lambda a: jnp.asarray(a, F32)
    return f(np.cos(a1)), f(np.sin(a1)), f(np.cos(a2)), f(np.sin(a2)), f(cs)


def kernel(x, c, ctx, c_ctx, w_ada, b_ada, norm1_g, w_in, q_norm_g, k_norm_g, w_fourier,
           w_out, norm2_g, w_mlp1, w_mlp2, final_norm_g):
    b, s, d = x.shape
    assert w_ada.shape[0] == 1, "single-layer problem: the context stream is never updated"
    assert d == D_MODEL and s % GRID_W == 0

    c_rows = jnp.concatenate([c, c_ctx[None, :], jnp.zeros((MOD_ROWS - b - 1, d), F32)], axis=0)
    mod = _ada(c_rows, w_ada[0], b_ada)
    mod3 = mod.reshape(MOD_ROWS, 1, N_MOD * d)

    cos, sin = _rope_tables(s)
    c1, s1, c2, s2, cs = _dft_tables(s)
    w_in_b = w_in[0].astype(BF16)
    w_kv_b = w_in_b[:, ATTN_WIDTH:ATTN_WIDTH + 2 * KV_WIDTH]
    g1 = norm1_g
    qg = q_norm_g
    kg = k_norm_g

    ab = _fourier_weights(cs, w_fourier[0], s)
    kc, vc = _ctx_kv(ctx, mod3, g1, kg, w_kv_b)
    q, k, v, t = _in_proj(x, mod3, g1, cos, sin, qg, kg, w_in_b, ab, tm=512)
    attn = _attention(q, kc, k, vc, v, tq=256)
    four = _dft(c1, s1, c2, s2, t.reshape(b, 2 * s, FOURIER_WIDTH), tn=256)
    x1, h2 = _out_proj(attn, four, x, mod3, norm2_g, w_out[0].astype(BF16), tm=512)
    return _mlp(h2, x1, mod3, final_norm_g[None, :], w_mlp1[0].astype(BF16), w_mlp2[0].astype(BF16),
                tm=512, tf=512)
```

```python
import functools

import numpy as np
import jax
import jax.numpy as jnp
from jax import lax
from jax.experimental import pallas as pl
from jax.experimental.pallas import tpu as pltpu

F32 = jnp.float32
BF16 = jnp.bfloat16

D_MODEL = 2048
GRID_W = 64
HEAD_DIM = 128
N_Q_HEADS = 12
N_KV_HEADS = 4
Q_PER_KV = N_Q_HEADS // N_KV_HEADS
N_GROUPS = 4
GROUP_DIM = 128
ATTN_WIDTH = N_Q_HEADS * HEAD_DIM
KV_WIDTH = N_KV_HEADS * HEAD_DIM
FOURIER_WIDTH = N_GROUPS * GROUP_DIM
IN_WIDTH = ATTN_WIDTH + 2 * KV_WIDTH + FOURIER_WIDTH
D_FF = 4 * D_MODEL
ROPE_THETA = 10000.0
ROPE_AXIS_DIM = HEAD_DIM // 2
N_MOD = 6
EPS = 1e-6
LOG2_E = 1.4426950408889634

V7X_VMEM_BYTES = 64 * 1024 * 1024
VMEM_LIMIT = V7X_VMEM_BYTES * 7 // 8
MOD_ROWS = 8
CTX_MOD_ROW = 4
DFT_LO = 32


def _params(*sem):
    return pltpu.CompilerParams(dimension_semantics=sem, vmem_limit_bytes=VMEM_LIMIT)


def _resident(shape):
    return pl.BlockSpec(shape, lambda *_: (0,) * len(shape), pipeline_mode=pl.Buffered(1))


def _dot(a, b):
    return jnp.dot(a, b, preferred_element_type=F32)


def _rms(x):
    return x * lax.rsqrt(jnp.mean(x * x, axis=-1, keepdims=True) + EPS)


def _ada_kernel(c_ref, w_ref, b_ref, o_ref):
    c = c_ref[...]
    a = c / (1.0 + jnp.exp(-c))
    o_ref[...] = _dot(a.astype(BF16), w_ref[...].astype(BF16)) + b_ref[...]


def _ada(c_rows, w_ada, b_ada):
    tn = 1024
    n = w_ada.shape[1]
    return pl.pallas_call(
        _ada_kernel,
        out_shape=jax.ShapeDtypeStruct((MOD_ROWS, n), F32),
        grid=(n // tn,),
        in_specs=[pl.BlockSpec((MOD_ROWS, D_MODEL), lambda j: (0, 0)),
                  pl.BlockSpec((D_MODEL, tn), lambda j: (0, j)),
                  pl.BlockSpec((1, tn), lambda j: (0, j))],
        out_specs=pl.BlockSpec((MOD_ROWS, tn), lambda j: (0, j)),
        compiler_params=_params("parallel"),
        name="ada",
    )(c_rows, w_ada, b_ada)


def _fw_kernel(cs_ref, wf_ref, o_ref, *, scale):
    r = _dot(cs_ref[...].astype(BF16), wf_ref[0].astype(BF16)) * scale
    o_ref[0, :, :GROUP_DIM] = r[:GROUP_DIM].astype(BF16)
    o_ref[0, :, GROUP_DIM:] = r[GROUP_DIM:].astype(BF16)


def _fourier_weights(cs, w_f, seq):
    scale = 1.0 / float(np.sqrt(seq * GROUP_DIM))
    return pl.pallas_call(
        functools.partial(_fw_kernel, scale=scale),
        out_shape=jax.ShapeDtypeStruct((N_GROUPS, GROUP_DIM, 2 * GROUP_DIM), BF16),
        grid=(N_GROUPS,),
        in_specs=[pl.BlockSpec((2 * GROUP_DIM, GROUP_DIM), lambda g: (0, 0)),
                  pl.BlockSpec((1, GROUP_DIM, GROUP_DIM), lambda g: (g, 0, 0))],
        out_specs=pl.BlockSpec((1, GROUP_DIM, 2 * GROUP_DIM), lambda g: (g, 0, 0)),
        compiler_params=_params("parallel"),
        name="fourier_weights",
    )(cs, w_f)


def _norm_modulate(x, g, shift, scale):
    return (_rms(x) * g) * (1.0 + scale) + shift


def _store_v_with_ones(v_ref, pv):
    ones = jnp.ones((pv.shape[0], HEAD_DIM), BF16)
    for j in range(N_KV_HEADS):
        v_ref[0, :, 2 * j * HEAD_DIM:(2 * j + 1) * HEAD_DIM] = pv[:, j * HEAD_DIM:(j + 1) * HEAD_DIM].astype(BF16)
        v_ref[0, :, (2 * j + 1) * HEAD_DIM:(2 * j + 2) * HEAD_DIM] = ones


def _ctx_kv_kernel(x_ref, sh_ref, sc_ref, g1_ref, kg_ref, w_ref, k_ref, v_ref):
    h = _norm_modulate(x_ref[0], g1_ref[...], sh_ref[0], sc_ref[0]).astype(BF16)
    pk = _dot(h, w_ref[:, :KV_WIDTH])
    kg = kg_ref[...]
    for j in range(N_KV_HEADS):
        sl = slice(j * HEAD_DIM, (j + 1) * HEAD_DIM)
        k_ref[0, :, sl] = (_rms(pk[:, sl]) * kg).astype(BF16)
    _store_v_with_ones(v_ref, _dot(h, w_ref[:, KV_WIDTH:]))


def _ctx_kv(ctx, mod3, g1, kg, w_kv):
    b, l, d = ctx.shape
    return pl.pallas_call(
        _ctx_kv_kernel,
        out_shape=(jax.ShapeDtypeStruct((b, l, KV_WIDTH), BF16),
                   jax.ShapeDtypeStruct((b, l, 2 * KV_WIDTH), BF16)),
        grid=(b,),
        in_specs=[pl.BlockSpec((1, l, d), lambda i: (i, 0, 0)),
                  pl.BlockSpec((1, 1, d), lambda i: (CTX_MOD_ROW, 0, 0)),
                  pl.BlockSpec((1, 1, d), lambda i: (CTX_MOD_ROW, 0, 1)),
                  _resident((1, d)),
                  _resident((1, HEAD_DIM)),
                  _resident((d, 2 * KV_WIDTH))],
        out_specs=(pl.BlockSpec((1, l, KV_WIDTH), lambda i: (i, 0, 0)),
                   pl.BlockSpec((1, l, 2 * KV_WIDTH), lambda i: (i, 0, 0))),
        compiler_params=_params("parallel"),
        name="ctx_kv",
    )(ctx, mod3, mod3, g1, kg, w_kv)


def _in_proj_kernel(x_ref, sh_ref, sc_ref, g1_ref, cos_ref, sin_ref, qg_ref, kg_ref, w_ref, ab_ref,
                    q_ref, k_ref, v_ref, t_ref):
    h = _norm_modulate(x_ref[0], g1_ref[...], sh_ref[0], sc_ref[0]).astype(BF16)
    cos = cos_ref[...]
    sin = sin_ref[...]
    lane = lax.broadcasted_iota(jnp.int32, cos.shape, 1)
    first_half = (lane % ROPE_AXIS_DIM) < (ROPE_AXIS_DIM // 2)

    def norm_rope(p, g):
        y = _rms(p) * g
        partner = jnp.where(first_half,
                            pltpu.roll(y, HEAD_DIM - ROPE_AXIS_DIM // 2, 1),
                            pltpu.roll(y, ROPE_AXIS_DIM // 2, 1))
        return y * cos + partner * sin

    heads_per_chunk = KV_WIDTH // HEAD_DIM
    qg = qg_ref[...] * (HEAD_DIM ** -0.5 * LOG2_E)
    for c in range(ATTN_WIDTH // KV_WIDTH):
        p = _dot(h, w_ref[:, c * KV_WIDTH:(c + 1) * KV_WIDTH])
        for j in range(heads_per_chunk):
            sl = slice(j * HEAD_DIM, (j + 1) * HEAD_DIM)
            q_ref[0, c * heads_per_chunk + j] = norm_rope(p[:, sl], qg).astype(BF16)
    p = _dot(h, w_ref[:, ATTN_WIDTH:ATTN_WIDTH + KV_WIDTH])
    kg = kg_ref[...]
    for j in range(N_KV_HEADS):
        sl = slice(j * HEAD_DIM, (j + 1) * HEAD_DIM)
        k_ref[0, :, sl] = norm_rope(p[:, sl], kg).astype(BF16)
    _store_v_with_ones(v_ref, _dot(h, w_ref[:, ATTN_WIDTH + KV_WIDTH:ATTN_WIDTH + 2 * KV_WIDTH]))
    u =_dot(h, w_ref[:, ATTN_WIDTH + 2 * KV_WIDTH:]).astype(BF16)
    for g in range(N_GROUPS):
        sl = slice(g * GROUP_DIM, (g + 1) * GROUP_DIM)
        t = _dot(u[:, sl], ab_ref[g])
        t_ref[0, 0, :, sl] = t[:, :GROUP_DIM].astype(BF16)
        t_ref[0, 1, :, sl] = t[:, GROUP_DIM:].astype(BF16)


def _in_proj(x, mod3, g1, cos, sin, qg, kg, w_in, ab, tm):
    b, s, d = x.shape
    row = lambda i, t: (i, t, 0)
    return pl.pallas_call(
        _in_proj_kernel,
        out_shape=(jax.ShapeDtypeStruct((b, N_Q_HEADS, s, HEAD_DIM), BF16),
                   jax.ShapeDtypeStruct((b, s, KV_WIDTH), BF16),
                   jax.ShapeDtypeStruct((b, s, 2 * KV_WIDTH), BF16),
                   jax.ShapeDtypeStruct((b, 2, s, FOURIER_WIDTH), BF16)),
        grid=(b, s // tm),
        in_specs=[pl.BlockSpec((1, tm, d), row),
                  pl.BlockSpec((1, 1, d), lambda i, t: (i, 0, 0)),
                  pl.BlockSpec((1, 1, d), lambda i, t: (i, 0, 1)),
                  _resident((1, d)),
                  pl.BlockSpec((tm, HEAD_DIM), lambda i, t: (t, 0)),
                  pl.BlockSpec((tm, HEAD_DIM), lambda i, t: (t, 0)),
                  _resident((1, HEAD_DIM)),
                  _resident((1, HEAD_DIM)),
                  _resident((d, IN_WIDTH)),
                  _resident((N_GROUPS, GROUP_DIM, 2 * GROUP_DIM))],
        out_specs=(pl.BlockSpec((1, N_Q_HEADS, tm, HEAD_DIM), lambda i, t: (i, 0, t, 0)),
                   pl.BlockSpec((1, tm, KV_WIDTH), row),
                   pl.BlockSpec((1, tm, 2 * KV_WIDTH), row),
                   pl.BlockSpec((1, 2, tm, FOURIER_WIDTH), lambda i, t: (i, 0, t, 0))),
        compiler_params=_params("parallel", "parallel"),
        name="in_proj",
    )(x, mod3, mod3, g1, cos, sin, qg, kg, w_in, ab)


def _attn_kernel(q_ref, kc_ref, k_ref, vc_ref, v_ref, o_ref, *, chunk):
    nt = (((1,), (1,)), ((), ()))
    tq = q_ref.shape[2]
    q = q_ref[0].reshape(Q_PER_KV * tq, HEAD_DIM)
    blocks = [(kc_ref, vc_ref, 0, kc_ref.shape[1])]
    blocks += [(k_ref, v_ref, i * chunk, chunk) for i in range(k_ref.shape[1] // chunk)]
    m = acc = None
    for kr, vr, start, size in blocks:
        s = lax.dot_general(q, kr[0, start:start + size, :], nt, preferred_element_type=F32)
        m_blk = jnp.max(s, axis=-1, keepdims=True)
        m_new = m_blk if m is None else jnp.maximum(m, m_blk)
        p = jnp.exp2(s - m_new).astype(BF16)
        pv = _dot(p, vr[0, start:start + size, :])
        acc = pv if m is None else jnp.exp2(m - m_new) * acc + pv
        m = m_new
    o = acc[:, :HEAD_DIM] / acc[:, HEAD_DIM:]
    for g in range(Q_PER_KV):
        o_ref[0, :, g * HEAD_DIM:(g + 1) * HEAD_DIM] = o[g * tq:(g + 1) * tq].astype(BF16)


def _attention(q, kc, k, vc, v, tq, chunk):
    b, _, s, _ = q.shape
    l = kc.shape[1]
    gw = Q_PER_KV * HEAD_DIM
    kspec = lambda n: pl.BlockSpec((1, n, HEAD_DIM), lambda i, h, t: (i, 0, h))
    vspec = lambda n: pl.BlockSpec((1, n, 2 * HEAD_DIM), lambda i, h, t: (i, 0, h))
    return pl.pallas_call(
        functools.partial(_attn_kernel, chunk=chunk),
        out_shape=jax.ShapeDtypeStruct((b, s, ATTN_WIDTH), BF16),
        grid=(b, N_KV_HEADS, s // tq),
        in_specs=[pl.BlockSpec((1, Q_PER_KV, tq, HEAD_DIM), lambda i, h, t: (i, h, t, 0)),
                  kspec(l), kspec(s), vspec(l), vspec(s)],
        out_specs=pl.BlockSpec((1, tq, gw), lambda i, h, t: (i, t, h)),
        compiler_params=_params("parallel", "parallel", "parallel"),
        name="attn",
    )(q, kc, k, vc, v)


def _dft_kernel(c1_ref, s1_ref, c2_ref, s2_ref, t_ref, o_ref, tab_ref):
    seq = c2_ref.shape[1]
    c2 = c2_ref[...]
    s2 = s2_ref[...]
    for r in range(c1_ref.shape[0]):
        c1 = c1_ref[r:r + 1, :]
        s1 = s1_ref[r:r + 1, :]
        rows = slice(r * DFT_LO, (r + 1) * DFT_LO)
        tab_ref[rows, :seq] = (c1 * c2 - s1 * s2).astype(BF16)
        tab_ref[rows, seq:] = (-(s1 * c2 + c1 * s2)).astype(BF16)
    tab = tab_ref[...]
    for i in range(t_ref.shape[0]):
        o_ref[i] = _dot(tab, t_ref[i]).astype(BF16)


def _dft(c1, s1, c2, s2, t, tn):
    b, two_s, w = t.shape
    s = two_s // 2
    hi = tn // DFT_LO
    return pl.pallas_call(
        _dft_kernel,
        out_shape=jax.ShapeDtypeStruct((b, s, w), BF16),
        grid=(s // tn,),
        in_specs=[pl.BlockSpec((hi, s), lambda i: (i, 0)),
                  pl.BlockSpec((hi, s), lambda i: (i, 0)),
                  _resident((DFT_LO, s)),
                  _resident((DFT_LO, s)),
                  _resident((b, two_s, w))],
        out_specs=pl.BlockSpec((b, tn, w), lambda i: (0, i, 0)),
        scratch_shapes=[pltpu.VMEM((tn, two_s), BF16)],
        compiler_params=_params("parallel"),
        name="dft",
    )(c1, s1, c2, s2, t)


def _out_proj_kernel(a_ref, f_ref, x_ref, gt_ref, sh_ref, sc_ref, g2_ref, w_ref, x1_ref, h2_ref):
    y = _dot(a_ref[0], w_ref[:ATTN_WIDTH, :]) + _dot(f_ref[0], w_ref[ATTN_WIDTH:, :])
    x1 = x_ref[0] + gt_ref[0] * y
    x1_ref[0] = x1
    h2_ref[0] = _norm_modulate(x1, g2_ref[...], sh_ref[0], sc_ref[0]).astype(BF16)


def _out_proj(attn, four, x, mod3, g2, w_out, tm):
    b, s, d = x.shape
    row = lambda i, t: (i, t, 0)
    mod = lambda j: pl.BlockSpec((1, 1, d), lambda i, t: (i, 0, j))
    return pl.pallas_call(
        _out_proj_kernel,
        out_shape=(jax.ShapeDtypeStruct((b, s, d), F32),
                   jax.ShapeDtypeStruct((b, s, d), BF16)),
        grid=(b, s // tm),
        in_specs=[pl.BlockSpec((1, tm, ATTN_WIDTH), row),
                  pl.BlockSpec((1, tm, FOURIER_WIDTH), row),
                  pl.BlockSpec((1, tm, d), row),
                  mod(2), mod(3), mod(4),
                  _resident((1, d)),
                  _resident((ATTN_WIDTH + FOURIER_WIDTH, d))],
        out_specs=(pl.BlockSpec((1, tm, d), row),
                   pl.BlockSpec((1, tm, d), row)),
        compiler_params=_params("parallel", "parallel"),
        name="out_proj",
    )(attn, four, x, mod3, mod3, mod3, g2, w_out)


def _mlp_kernel(h_ref, x1_hbm, gt_ref, gf_ref, w1_ref, w2_ref, o_ref, sem):
    i, t, f = pl.program_id(0), pl.program_id(1), pl.program_id(2)
    tm = o_ref.shape[1]
    seed = pltpu.make_async_copy(x1_hbm.at[i, pl.ds(pl.multiple_of(t * tm, tm), tm), :], o_ref.at[0], sem)

    @pl.when(f == 0)
    def _():
        seed.start()

    a = jnp.maximum(_dot(h_ref[0], w1_ref[...]), 0.0)
    a = (a * a).astype(BF16)

    @pl.when(f == 0)
    def _():
        seed.wait()

    o_ref[0] += gt_ref[0] * _dot(a, w2_ref[...])

    @pl.when(f == pl.num_programs(2) - 1)
    def _():
        o_ref[0] = _rms(o_ref[0]) * gf_ref[...]


def _mlp(h2, x1, mod3, gf, w1, w2, tm, tf):
    b, s, d = x1.shape
    row = lambda i, t, f: (i, t, 0)
    return pl.pallas_call(
        _mlp_kernel,
        out_shape=jax.ShapeDtypeStruct((b, s, d), F32),
        grid=(b, s // tm, D_FF // tf),
        in_specs=[pl.BlockSpec((1, tm, d), row),
                  pl.BlockSpec(memory_space=pl.ANY),
                  pl.BlockSpec((1, 1, d), lambda i, t, f: (i, 0, 5)),
                  pl.BlockSpec((1, d), lambda i, t, f: (0, 0)),
                  pl.BlockSpec((d, tf), lambda i, t, f: (0, f)),
                  pl.BlockSpec((tf, d), lambda i, t, f: (f, 0))],
        out_specs=pl.BlockSpec((1, tm, d), row),
        scratch_shapes=[pltpu.SemaphoreType.DMA(())],
        compiler_params=_params("parallel", "parallel", "arbitrary"),
        name="mlp",
    )(h2, x1, mod3, gf, w1, w2)


def _rope_tables(seq):
    t = np.arange(seq)
    inv = ROPE_THETA ** (-np.arange(0, ROPE_AXIS_DIM, 2, dtype=np.float64) / ROPE_AXIS_DIM)
    ang_r = (t // GRID_W)[:, None] * inv
    ang_c = (t % GRID_W)[:, None] * inv
    cos = np.concatenate([np.cos(ang_r), np.cos(ang_r), np.cos(ang_c), np.cos(ang_c)], axis=-1)
    sin = np.concatenate([-np.sin(ang_r), np.sin(ang_r), -np.sin(ang_c), np.sin(ang_c)], axis=-1)
    return jnp.asarray(cos, F32), jnp.asarray(sin, F32)


def _dft_tables(seq):
    m = np.arange(seq)
    a1 = 2.0 * np.pi * ((DFT_LO * np.arange(seq // DFT_LO)[:, None] * m) % seq) / seq
    a2 = 2.0 * np.pi * ((np.arange(DFT_LO)[:, None] * m) % seq) / seq
    c = np.arange(GROUP_DIM)
    ac = 2.0 * np.pi * ((c[:, None] * c) % GROUP_DIM) / GROUP_DIM
    cs = np.concatenate([np.cos(ac), np.sin(ac)], axis=0)
    f = lambda a: jnp.asarray(a, F32)
    return f(np.cos(a1)), f(np.sin(a1)), f(np.cos(a2)), f(np.sin(a2)), f(cs)


def kernel(x, c, ctx, c_ctx, w_ada, b_ada, norm1_g, w_in, q_norm_g, k_norm_g, w_fourier,
           w_out, norm2_g, w_mlp1, w_mlp2, final_norm_g):
    b, s, d = x.shape
    assert w_ada.shape[0] == 1, "single-layer problem: the context stream is never updated"
    assert d == D_MODEL and s % GRID_W == 0

    c_rows = jnp.concatenate([c, c_ctx[None, :], jnp.zeros((MOD_ROWS - b - 1, d), F32)], axis=0)
    mod = _ada(c_rows, w_ada[0], b_ada)
    mod3 = mod.reshape(MOD_ROWS, 1, N_MOD * d)

    cos, sin = _rope_tables(s)
    c1, s1, c2, s2, cs = _dft_tables(s)
    w_in_b = w_in[0].astype(BF16)
    w_kv_b = w_in_b[:, ATTN_WIDTH:ATTN_WIDTH + 2 * KV_WIDTH]
    g1 = norm1_g
    qg = q_norm_g
    kg = k_norm_g

    ab = _fourier_weights(cs, w_fourier[0], s)
    kc, vc = _ctx_kv(ctx, mod3, g1, kg, w_kv_b)
    q, k, v, t = _in_proj(x, mod3, g1, cos, sin, qg, kg, w_in_b, ab, tm=512)
    attn = _attention(q, kc, k, vc, v, tq=256, chunk=512)
    four = _dft(c1, s1, c2, s2, t.reshape(b, 2 * s, FOURIER_WIDTH), tn=256)
    x1, h2 = _out_proj(attn, four, x, mod3, norm2_g, w_out[0].astype(BF16), tm=512)
    return _mlp(h2, x1, mod3, final_norm_g[None, :], w_mlp1[0].astype(BF16), w_mlp2[0].astype(BF16),
                tm=1024, tf=1024)
```

```python
import functools

import numpy as np
import jax
import jax.numpy as jnp
from jax import lax
from jax.experimental import pallas as pl
from jax.experimental.pallas import tpu as pltpu

F32 = jnp.float32
BF16 = jnp.bfloat16

D_MODEL = 2048
GRID_W = 64
HEAD_DIM = 128
N_Q_HEADS = 12
N_KV_HEADS = 4
Q_PER_KV = N_Q_HEADS // N_KV_HEADS
N_GROUPS = 4
GROUP_DIM = 128
ATTN_WIDTH = N_Q_HEADS * HEAD_DIM
KV_WIDTH = N_KV_HEADS * HEAD_DIM
FOURIER_WIDTH = N_GROUPS * GROUP_DIM
IN_WIDTH = ATTN_WIDTH + 2 * KV_WIDTH + FOURIER_WIDTH
D_FF = 4 * D_MODEL
ROPE_THETA = 10000.0
ROPE_AXIS_DIM = HEAD_DIM // 2
N_MOD = 6
EPS = 1e-6
LOG2_E = 1.4426950408889634

V7X_VMEM_BYTES = 64 * 1024 * 1024
VMEM_LIMIT = V7X_VMEM_BYTES * 7 // 8
MOD_ROWS = 8
CTX_MOD_ROW = 4
DFT_LO = 32


def _params(*sem):
    return pltpu.CompilerParams(dimension_semantics=sem, vmem_limit_bytes=VMEM_LIMIT)


def _resident(shape):
    return pl.BlockSpec(shape, lambda *_: (0,) * len(shape), pipeline_mode=pl.Buffered(1))


def _dot(a, b):
    return jnp.dot(a, b, preferred_element_type=F32)


def _rms(x):
    return x * lax.rsqrt(jnp.mean(x * x, axis=-1, keepdims=True) + EPS)


def _ada_kernel(c_ref, w_ref, b_ref, o_ref):
    c = c_ref[...]
    a = c / (1.0 + jnp.exp(-c))
    o_ref[...] = _dot(a.astype(BF16), w_ref[...].astype(BF16)) + b_ref[...]


def _ada(c_rows, w_ada, b_ada):
    tn = 1024
    n = w_ada.shape[1]
    return pl.pallas_call(
        _ada_kernel,
        out_shape=jax.ShapeDtypeStruct((MOD_ROWS, n), F32),
        grid=(n // tn,),
        in_specs=[pl.BlockSpec((MOD_ROWS, D_MODEL), lambda j: (0, 0)),
                  pl.BlockSpec((D_MODEL, tn), lambda j: (0, j)),
                  pl.BlockSpec((1, tn), lambda j: (0, j))],
        out_specs=pl.BlockSpec((MOD_ROWS, tn), lambda j: (0, j)),
        compiler_params=_params("parallel"),
        name="ada",
    )(c_rows, w_ada, b_ada)


def _fw_kernel(cs_ref, wf_ref, o_ref, *, scale):
    r = _dot(cs_ref[...].astype(BF16), wf_ref[0].astype(BF16)) * scale
    o_ref[0, :, :GROUP_DIM] = r[:GROUP_DIM].astype(BF16)
    o_ref[0, :, GROUP_DIM:] = r[GROUP_DIM:].astype(BF16)


def _fourier_weights(cs, w_f, seq):
    scale = 1.0 / float(np.sqrt(seq * GROUP_DIM))
    return pl.pallas_call(
        functools.partial(_fw_kernel, scale=scale),
        out_shape=jax.ShapeDtypeStruct((N_GROUPS, GROUP_DIM, 2 * GROUP_DIM), BF16),
        grid=(N_GROUPS,),
        in_specs=[pl.BlockSpec((2 * GROUP_DIM, GROUP_DIM), lambda g: (0, 0)),
                  pl.BlockSpec((1, GROUP_DIM, GROUP_DIM), lambda g: (g, 0, 0))],
        out_specs=pl.BlockSpec((1, GROUP_DIM, 2 * GROUP_DIM), lambda g: (g, 0, 0)),
        compiler_params=_params("parallel"),
        name="fourier_weights",
    )(cs, w_f)


def _norm_modulate(x, g, shift, scale):
    return (_rms(x) * g) * (1.0 + scale) + shift


def _store_v_with_ones(v_ref, pv):
    ones = jnp.ones((pv.shape[0], HEAD_DIM), BF16)
    for j in range(N_KV_HEADS):
        v_ref[0, :, 2 * j * HEAD_DIM:(2 * j + 1) * HEAD_DIM] = pv[:, j * HEAD_DIM:(j + 1) * HEAD_DIM].astype(BF16)
        v_ref[0, :, (2 * j + 1) * HEAD_DIM:(2 * j + 2) * HEAD_DIM] = ones


def _ctx_kv_kernel(x_ref, sh_ref, sc_ref, g1_ref, kg_ref, w_ref, k_ref, v_ref):
    h = _norm_modulate(x_ref[0], g1_ref[...], sh_ref[0], sc_ref[0]).astype(BF16)
    pk = _dot(h, w_ref[:, :KV_WIDTH])
    kg = kg_ref[...]
    for j in range(N_KV_HEADS):
        sl = slice(j * HEAD_DIM, (j + 1) * HEAD_DIM)
        k_ref[0, :, sl] = (_rms(pk[:, sl]) * kg).astype(BF16)
    _store_v_with_ones(v_ref, _dot(h, w_ref[:, KV_WIDTH:]))


def _ctx_kv(ctx, mod3, g1, kg, w_kv):
    b, l, d = ctx.shape
    return pl.pallas_call(
        _ctx_kv_kernel,
        out_shape=(jax.ShapeDtypeStruct((b, l, KV_WIDTH), BF16),
                   jax.ShapeDtypeStruct((b, l, 2 * KV_WIDTH), BF16)),
        grid=(b,),
        in_specs=[pl.BlockSpec((1, l, d), lambda i: (i, 0, 0)),
                  pl.BlockSpec((1, 1, d), lambda i: (CTX_MOD_ROW, 0, 0)),
                  pl.BlockSpec((1, 1, d), lambda i: (CTX_MOD_ROW, 0, 1)),
                  _resident((1, d)),
                  _resident((1, HEAD_DIM)),
                  _resident((d, 2 * KV_WIDTH))],
        out_specs=(pl.BlockSpec((1, l, KV_WIDTH), lambda i: (i, 0, 0)),
                   pl.BlockSpec((1, l, 2 * KV_WIDTH), lambda i: (i, 0, 0))),
        compiler_params=_params("parallel"),
        name="ctx_kv",
    )(ctx, mod3, mod3, g1, kg, w_kv)


def _in_proj_kernel(x_ref, sh_ref, sc_ref, g1_ref, cos_ref, sin_ref, qg_ref, kg_ref, w_ref, ab_ref,
                    q_ref, k_ref, v_ref, t_ref):
    h = _norm_modulate(x_ref[0], g1_ref[...], sh_ref[0], sc_ref[0]).astype(BF16)
    cos = cos_ref[...]
    sin = sin_ref[...]
    lane = lax.broadcasted_iota(jnp.int32, cos.shape, 1)
    first_half = (lane % ROPE_AXIS_DIM) < (ROPE_AXIS_DIM // 2)

    def norm_rope(p, g):
        y = _rms(p) * g
        partner = jnp.where(first_half,
                            pltpu.roll(y, HEAD_DIM - ROPE_AXIS_DIM // 2, 1),
                            pltpu.roll(y, ROPE_AXIS_DIM // 2, 1))
        return y * cos + partner * sin

    heads_per_chunk = KV_WIDTH // HEAD_DIM
    qg = qg_ref[...] * (HEAD_DIM ** -0.5 * LOG2_E)
    for c in range(ATTN_WIDTH // KV_WIDTH):
        p = _dot(h, w_ref[:, c * KV_WIDTH:(c + 1) * KV_WIDTH])
        for j in range(heads_per_chunk):
            sl = slice(j * HEAD_DIM, (j + 1) * HEAD_DIM)
            q_ref[0, c * heads_per_chunk + j] = norm_rope(p[:, sl], qg).astype(BF16)
    p = _dot(h, w_ref[:, ATTN_WIDTH:ATTN_WIDTH + KV_WIDTH])
    kg = kg_ref[...]
    for j in range(N_KV_HEADS):
        sl = slice(j * HEAD_DIM, (j + 1) * HEAD_DIM)
        k_ref[0, :, sl] = norm_rope(p[:, sl], kg).astype(BF16)
    _store_v_with_ones(v_ref, _dot(h, w_ref[:, ATTN_WIDTH + KV_WIDTH:ATTN_WIDTH + 2 * KV_WIDTH]))
    u =_dot(h, w_ref[:, ATTN_WIDTH + 2 * KV_WIDTH:]).astype(BF16)
    for g in range(N_GROUPS):
        sl = slice(g * GROUP_DIM, (g + 1) * GROUP_DIM)
        t = _dot(u[:, sl], ab_ref[g])
        t_ref[0, 0, :, sl] = t[:, :GROUP_DIM].astype(BF16)
        t_ref[0, 1, :, sl] = t[:, GROUP_DIM:].astype(BF16)


def _in_proj(x, mod3, g1, cos, sin, qg, kg, w_in, ab, tm):
    b, s, d = x.shape
    row = lambda i, t: (i, t, 0)
    return pl.pallas_call(
        _in_proj_kernel,
        out_shape=(jax.ShapeDtypeStruct((b, N_Q_HEADS, s, HEAD_DIM), BF16),
                   jax.ShapeDtypeStruct((b, s, KV_WIDTH), BF16),
                   jax.ShapeDtypeStruct((b, s, 2 * KV_WIDTH), BF16),
                   jax.ShapeDtypeStruct((b, 2, s, FOURIER_WIDTH), BF16)),
        grid=(b, s // tm),
        in_specs=[pl.BlockSpec((1, tm, d), row),
                  pl.BlockSpec((1, 1, d), lambda i, t: (i, 0, 0)),
                  pl.BlockSpec((1, 1, d), lambda i, t: (i, 0, 1)),
                  _resident((1, d)),
                  pl.BlockSpec((tm, HEAD_DIM), lambda i, t: (t, 0)),
                  pl.BlockSpec((tm, HEAD_DIM), lambda i, t: (t, 0)),
                  _resident((1, HEAD_DIM)),
                  _resident((1, HEAD_DIM)),
                  _resident((d, IN_WIDTH)),
                  _resident((N_GROUPS, GROUP_DIM, 2 * GROUP_DIM))],
        out_specs=(pl.BlockSpec((1, N_Q_HEADS, tm, HEAD_DIM), lambda i, t: (i, 0, t, 0)),
                   pl.BlockSpec((1, tm, KV_WIDTH), row),
                   pl.BlockSpec((1, tm, 2 * KV_WIDTH), row),
                   pl.BlockSpec((1, 2, tm, FOURIER_WIDTH), lambda i, t: (i, 0, t, 0))),
        compiler_params=_params("parallel", "parallel"),
        name="in_proj",
    )(x, mod3, mod3, g1, cos, sin, qg, kg, w_in, ab)


def _attn_kernel(q_ref, kc_ref, k_ref, vc_ref, v_ref, *refs, chunk, n_cast):
    o_ref = refs[n_cast]
    for src, dst in zip(refs[:n_cast], refs[n_cast + 1:]):
        dst[...] = src[...].astype(BF16)
    nt = (((1,), (1,)), ((), ()))
    tq = q_ref.shape[2]
    q = q_ref[0].reshape(Q_PER_KV * tq, HEAD_DIM)
    blocks = [(k_ref, v_ref, i * chunk, chunk) for i in range(k_ref.shape[1] // chunk)]
    blocks += [(kc_ref, vc_ref, 0, kc_ref.shape[1])]
    m = acc = None
    for kr, vr, start, size in blocks:
        s = lax.dot_general(q, kr[0, start:start + size, :], nt, preferred_element_type=F32)
        m_blk = jnp.max(s, axis=-1, keepdims=True)
        m_new = m_blk if m is None else jnp.maximum(m, m_blk)
        p = jnp.exp2(s - m_new).astype(BF16)
        pv = _dot(p, vr[0, start:start + size, :])
        acc = pv if m is None else jnp.exp2(m - m_new) * acc + pv
        m = m_new
    o = acc[:, :HEAD_DIM] / acc[:, HEAD_DIM:]
    for g in range(Q_PER_KV):
        o_ref[0, :, g * HEAD_DIM:(g + 1) * HEAD_DIM] = o[g * tq:(g + 1) * tq].astype(BF16)


def _attention(q, kc, k, vc, v, f32_weights, tq, chunk):
    b, _, s, _ = q.shape
    l = kc.shape[1]
    gw = Q_PER_KV * HEAD_DIM
    nt = s // tq
    steps = b * N_KV_HEADS * nt
    kspec = lambda n: pl.BlockSpec((1, n, HEAD_DIM), lambda i, h, t: (i, 0, h))
    vspec = lambda n: pl.BlockSpec((1, n, 2 * HEAD_DIM), lambda i, h, t: (i, 0, h))
    slab = lambda w: pl.BlockSpec((w.shape[0] // steps, w.shape[1]),
                                  lambda i, h, t: ((i * N_KV_HEADS + h) * nt + t, 0))
    out = pl.pallas_call(
        functools.partial(_attn_kernel, chunk=chunk, n_cast=len(f32_weights)),
        out_shape=(jax.ShapeDtypeStruct((b, s, ATTN_WIDTH), BF16),
                   *(jax.ShapeDtypeStruct(w.shape, BF16) for w in f32_weights)),
        grid=(b, N_KV_HEADS, nt),
        in_specs=[pl.BlockSpec((1, Q_PER_KV, tq, HEAD_DIM), lambda i, h, t: (i, h, t, 0)),
                  kspec(l), kspec(s), vspec(l), vspec(s), *(slab(w) for w in f32_weights)],
        out_specs=(pl.BlockSpec((1, tq, gw), lambda i, h, t: (i, t, h)),
                   *(slab(w) for w in f32_weights)),
        compiler_params=_params("arbitrary", "arbitrary", "arbitrary"),
        name="attn",
    )(q, kc, k, vc, v, *f32_weights)
    return out[0], out[1:]


def _dft_kernel(c1_ref, s1_ref, c2_ref, s2_ref, t_ref, o_ref, tab_ref):
    seq = c2_ref.shape[1]
    c2 = c2_ref[...]
    s2 = s2_ref[...]
    for r in range(c1_ref.shape[0]):
        c1 = c1_ref[r:r + 1, :]
        s1 = s1_ref[r:r + 1, :]
        rows = slice(r * DFT_LO, (r + 1) * DFT_LO)
        tab_ref[rows, :seq] = (c1 * c2 - s1 * s2).astype(BF16)
        tab_ref[rows, seq:] = (-(s1 * c2 + c1 * s2)).astype(BF16)
    tab = tab_ref[...]
    for i in range(t_ref.shape[0]):
        o_ref[i] = _dot(tab, t_ref[i]).astype(BF16)


def _dft(c1, s1, c2, s2, t, tn):
    b, two_s, w = t.shape
    s = two_s // 2
    hi = tn // DFT_LO
    return pl.pallas_call(
        _dft_kernel,
        out_shape=jax.ShapeDtypeStruct((b, s, w), BF16),
        grid=(s // tn,),
        in_specs=[pl.BlockSpec((hi, s), lambda i: (i, 0)),
                  pl.BlockSpec((hi, s), lambda i: (i, 0)),
                  _resident((DFT_LO, s)),
                  _resident((DFT_LO, s)),
                  _resident((b, two_s, w))],
        out_specs=pl.BlockSpec((b, tn, w), lambda i: (0, i, 0)),
        scratch_shapes=[pltpu.VMEM((tn, two_s), BF16)],
        compiler_params=_params("parallel"),
        name="dft",
    )(c1, s1, c2, s2, t)


def _out_proj_kernel(a_ref, f_ref, x_ref, gt_ref, sh_ref, sc_ref, g2_ref, w_ref, x1_ref, h2_ref):
    y = _dot(a_ref[0], w_ref[:ATTN_WIDTH, :]) + _dot(f_ref[0], w_ref[ATTN_WIDTH:, :])
    x1 = x_ref[0] + gt_ref[0] * y
    x1_ref[0] = x1
    h2_ref[0] = _norm_modulate(x1, g2_ref[...], sh_ref[0], sc_ref[0]).astype(BF16)


def _out_proj(attn, four, x, mod3, g2, w_out, tm):
    b, s, d = x.shape
    row = lambda i, t: (i, t, 0)
    mod = lambda j: pl.BlockSpec((1, 1, d), lambda i, t: (i, 0, j))
    return pl.pallas_call(
        _out_proj_kernel,
        out_shape=(jax.ShapeDtypeStruct((b, s, d), F32),
                   jax.ShapeDtypeStruct((b, s, d), BF16)),
        grid=(b, s // tm),
        in_specs=[pl.BlockSpec((1, tm, ATTN_WIDTH), row),
                  pl.BlockSpec((1, tm, FOURIER_WIDTH), row),
                  pl.BlockSpec((1, tm, d), row),
                  mod(2), mod(3), mod(4),
                  _resident((1, d)),
                  _resident((ATTN_WIDTH + FOURIER_WIDTH, d))],
        out_specs=(pl.BlockSpec((1, tm, d), row),
                   pl.BlockSpec((1, tm, d), row)),
        compiler_params=_params("parallel", "parallel"),
        name="out_proj",
    )(attn, four, x, mod3, mod3, mod3, g2, w_out)


def _mlp_kernel(h_ref, x1_hbm, gt_ref, gf_ref, w1_ref, w2_ref, o_ref, sem):
    i, t, f = pl.program_id(0), pl.program_id(1), pl.program_id(2)
    tm = o_ref.shape[1]
    seed = pltpu.make_async_copy(x1_hbm.at[i, pl.ds(pl.multiple_of(t * tm, tm), tm), :], o_ref.at[0], sem)

    @pl.when(f == 0)
    def _():
        seed.start()

    a = jnp.maximum(_dot(h_ref[0], w1_ref[...]), 0.0)
    a = (a * a).astype(BF16)

    @pl.when(f == 0)
    def _():
        seed.wait()

    o_ref[0] += gt_ref[0] * _dot(a, w2_ref[...])

    @pl.when(f == pl.num_programs(2) - 1)
    def _():
        o_ref[0] = _rms(o_ref[0]) * gf_ref[...]


def _mlp(h2, x1, mod3, gf, w1, w2, tm, tf):
    b, s, d = x1.shape
    row = lambda i, t, f: (i, t, 0)
    return pl.pallas_call(
        _mlp_kernel,
        out_shape=jax.ShapeDtypeStruct((b, s, d), F32),
        grid=(b, s // tm, D_FF // tf),
        in_specs=[pl.BlockSpec((1, tm, d), row),
                  pl.BlockSpec(memory_space=pl.ANY),
                  pl.BlockSpec((1, 1, d), lambda i, t, f: (i, 0, 5)),
                  pl.BlockSpec((1, d), lambda i, t, f: (0, 0)),
                  pl.BlockSpec((d, tf), lambda i, t, f: (0, f)),
                  pl.BlockSpec((tf, d), lambda i, t, f: (f, 0))],
        out_specs=pl.BlockSpec((1, tm, d), row),
        scratch_shapes=[pltpu.SemaphoreType.DMA(())],
        compiler_params=_params("parallel", "parallel", "arbitrary"),
        name="mlp",
    )(h2, x1, mod3, gf, w1, w2)


def _rope_tables(seq):
    t = np.arange(seq)
    inv = ROPE_THETA ** (-np.arange(0, ROPE_AXIS_DIM, 2, dtype=np.float64) / ROPE_AXIS_DIM)
    ang_r = (t // GRID_W)[:, None] * inv
    ang_c = (t % GRID_W)[:, None] * inv
    cos = np.concatenate([np.cos(ang_r), np.cos(ang_r), np.cos(ang_c), np.cos(ang_c)], axis=-1)
    sin = np.concatenate([-np.sin(ang_r), np.sin(ang_r), -np.sin(ang_c), np.sin(ang_c)], axis=-1)
    return jnp.asarray(cos, F32), jnp.asarray(sin, F32)


def _dft_tables(seq):
    m = np.arange(seq)
    a1 = 2.0 * np.pi * ((DFT_LO * np.arange(seq // DFT_LO)[:, None] * m) % seq) / seq
    a2 = 2.0 * np.pi * ((np.arange(DFT_LO)[:, None] * m) % seq) / seq
    c = np.arange(GROUP_DIM)
    ac = 2.0 * np.pi * ((c[:, None] * c) % GROUP_DIM) / GROUP_DIM
    cs = np.concatenate([np.cos(ac), np.sin(ac)], axis=0)
    f = lambda a: jnp.asarray(a, F32)
    return f(np.cos(a1)), f(np.sin(a1)), f(np.cos(a2)), f(np.sin(a2)), f(cs)


def kernel(x, c, ctx, c_ctx, w_ada, b_ada, norm1_g, w_in, q_norm_g, k_norm_g, w_fourier,
           w_out, norm2_g, w_mlp1, w_mlp2, final_norm_g):
    b, s, d = x.shape
    assert w_ada.shape[0] == 1, "single-layer problem: the context stream is never updated"
    assert d == D_MODEL and s % GRID_W == 0

    c_rows = jnp.concatenate([c, c_ctx[None, :], jnp.zeros((MOD_ROWS - b - 1, d), F32)], axis=0)
    mod = _ada(c_rows, w_ada[0], b_ada)
    mod3 = mod.reshape(MOD_ROWS, 1, N_MOD * d)

    cos, sin = _rope_tables(s)
    c1, s1, c2, s2, cs = _dft_tables(s)
    w_in_b = w_in[0].astype(BF16)
    w_kv_b = w_in_b[:, ATTN_WIDTH:ATTN_WIDTH + 2 * KV_WIDTH]
    g1 = norm1_g
    qg = q_norm_g
    kg = k_norm_g

    ab = _fourier_weights(cs, w_fourier[0], s)
    kc, vc = _ctx_kv(ctx, mod3, g1, kg, w_kv_b)
    q, k, v, t = _in_proj(x, mod3, g1, cos, sin, qg, kg, w_in_b, ab, tm=512)
    attn, (w_out_b, w1_b, w2_b) = _attention(q, kc, k, vc, v, (w_out[0], w_mlp1[0], w_mlp2[0]),
                                             tq=512, chunk=512)
    four = _dft(c1, s1, c2, s2, t.reshape(b, 2 * s, FOURIER_WIDTH), tn=256)
    x1, h2 = _out_proj(attn, four, x, mod3, norm2_g, w_out_b, tm=512)
    return _mlp(h2, x1, mod3, final_norm_g[None, :], w1_b, w2_b, tm=1024, tf=1024)
```
